```python
import math
import jax, jax.numpy as jnp
from jax import lax
import numpy as np

D_MODEL = 2048
BATCH = 2
SEQ = 8192
DEPTH = 4

N_MIXERS = 2
N_MOBA = (DEPTH + 1) // 2
N_RET = DEPTH // 2

MOBA_HEAD_DIM = 128
MOBA_HEADS = D_MODEL // MOBA_HEAD_DIM
MOBA_BLOCK = 256
MOBA_TOPK = 3
MOBA_Q_CHUNK = 32
ROPE_THETA = 10000.0

RET_HEADS = max(4, D_MODEL // 256)
RET_QK_DIM = D_MODEL // RET_HEADS
RET_V_DIM = 2 * D_MODEL // RET_HEADS
RET_CHUNK = 256
GN_EPS = 1e-5

FFN_DIM = 256 * ((8 * D_MODEL // 3 + 255) // 256)
CONV_WIDTH = 3

LN_EPS = 1e-5
DEEPNORM_ALPHA = (2 * DEPTH) ** 0.25
DEEPNORM_BETA = (8 * DEPTH) ** -0.25

PAD_MULT = math.lcm(MOBA_BLOCK, RET_CHUNK, MOBA_Q_CHUNK)

kernel_name = "moba_retnet_convffn_deepnorm_hybrid"


def rope_tables(n_pos, dim):
    inv = ROPE_THETA ** (-jnp.arange(0, dim, 2, dtype=jnp.float32) / dim)
    ang = jnp.arange(n_pos, dtype=jnp.float32)[:, None] * inv[None, :]
    return jnp.cos(ang), jnp.sin(ang)


def apply_rope(t, cos, sin):
    t1, t2 = jnp.split(t, 2, axis=-1)
    c = cos.astype(t.dtype)
    s = sin.astype(t.dtype)
    return jnp.concatenate([t1 * c - t2 * s, t1 * s + t2 * c], axis=-1)


def layer_norm(x, g, b):
    xf = x.astype(jnp.float32)
    mu = xf.mean(-1, keepdims=True)
    var = jnp.square(xf - mu).mean(-1, keepdims=True)
    return ((xf - mu) * lax.rsqrt(var + LN_EPS)).astype(x.dtype) * g + b


def moba_mixer(x, w_qkv, w_o, cos, sin):
    B, S, _ = x.shape
    H, Dh, BLK, C = MOBA_HEADS, MOBA_HEAD_DIM, MOBA_BLOCK, MOBA_Q_CHUNK
    qkv = jnp.einsum('bsd,de->bse', x, w_qkv).reshape(B, S, 3, H, Dh)
    q = jnp.transpose(qkv[:, :, 0], (0, 2, 1, 3))
    k = jnp.transpose(qkv[:, :, 1], (0, 2, 1, 3))
    v = jnp.transpose(qkv[:, :, 2], (0, 2, 1, 3))
    q = apply_rope(q, cos, sin) * (Dh ** -0.5)
    k = apply_rope(k, cos, sin)
    nb = S // BLK
    kb = k.reshape(B, H, nb, BLK, Dh)
    vb = v.reshape(B, H, nb, BLK, Dh)
    k_mean = kb.astype(jnp.float32).mean(axis=3)
    topk = min(MOBA_TOPK, nb)
    n_chunks = S // C
    q_chunks = q.reshape(B, H, n_chunks, C, Dh).transpose(2, 0, 1, 3, 4)
    b_idx = jnp.arange(B)[:, None, None, None]
    h_idx = jnp.arange(H)[None, :, None, None]

    def chunk_attn(args):
        q_c, c_idx = args
        start = c_idx * C
        blk = start // BLK
        gate = jnp.einsum('bhcd,bhnd->bhcn', q_c.astype(jnp.float32), k_mean)
        gate = jnp.where(jnp.arange(nb) < blk, gate, -jnp.inf)
        _, sel = lax.top_k(gate, topk)
        sel_valid = sel < blk
        k_sel = kb[b_idx, h_idx, sel]
        v_sel = vb[b_idx, h_idx, sel]
        l_sel = jnp.einsum('bhcd,bhctkd->bhctk', q_c, k_sel,
                           preferred_element_type=jnp.float32)
        l_sel = jnp.where(sel_valid[..., None], l_sel, -jnp.inf).reshape(B, H, C, topk * BLK)
        k_own = lax.dynamic_index_in_dim(kb, blk, axis=2, keepdims=False)
        v_own = lax.dynamic_index_in_dim(vb, blk, axis=2, keepdims=False)
        l_own = jnp.einsum('bhcd,bhkd->bhck', q_c, k_own,
                           preferred_element_type=jnp.float32)
        q_pos = start + jnp.arange(C)
        k_pos = blk * BLK + jnp.arange(BLK)
        l_own = jnp.where(k_pos[None, :] <= q_pos[:, None], l_own, -jnp.inf)
        p = jax.nn.softmax(jnp.concatenate([l_sel, l_own], axis=-1), axis=-1).astype(v.dtype)
        p_sel = p[..., :topk * BLK].reshape(B, H, C, topk, BLK)
        p_own = p[..., topk * BLK:]
        return (jnp.einsum('bhctk,bhctkd->bhcd', p_sel, v_sel)
                + jnp.einsum('bhck,bhkd->bhcd', p_own, v_own))

    o = lax.map(chunk_attn, (q_chunks, jnp.arange(n_chunks, dtype=jnp.int32)))
    o = o.transpose(1, 0, 3, 2, 4).reshape(B, S, H * Dh)
    return jnp.einsum('bse,ed->bsd', o, w_o)


def retention_mixer(x, w_q, w_k, w_v, w_g, w_o, cos, sin):
    B, S, _ = x.shape
    H, dk, dv, L = RET_HEADS, RET_QK_DIM, RET_V_DIM, RET_CHUNK
    nc = S // L

    def heads(t, d):
        return t.reshape(B, S, H, d).transpose(0, 2, 1, 3)

    q = apply_rope(heads(jnp.einsum('bsd,de->bse', x, w_q), dk), cos, sin)
    k = apply_rope(heads(jnp.einsum('bsd,de->bse', x, w_k), dk), cos, sin) * (dk ** -0.5)
    v = heads(jnp.einsum('bsd,de->bse', x, w_v), dv)
    log_gamma = jnp.log1p(-jnp.exp2(-5.0 - jnp.arange(H, dtype=jnp.float32)))
    pos = jnp.arange(L, dtype=jnp.float32)
    diff = pos[:, None] - pos[None, :]
    decay = jnp.where(diff >= 0,
                      jnp.exp(jnp.maximum(diff, 0.0)[None] * log_gamma[:, None, None]),
                      0.0)
    xi = jnp.exp((pos[None, :] + 1.0) * log_gamma[:, None])
    zeta = jnp.exp((L - 1.0 - pos[None, :]) * log_gamma[:, None])
    g_chunk = jnp.exp(L * log_gamma)

    qc = q.reshape(B, H, nc, L, dk)
    kc = k.reshape(B, H, nc, L, dk)
    vc = v.reshape(B, H, nc, L, dv)
    s = jnp.einsum('bhcnd,bhcmd->bhcnm', qc, kc,
                   preferred_element_type=jnp.float32) * decay[None, :, None]
    y_inner = jnp.einsum('bhcnm,bhcme->bhcne', s.astype(v.dtype), vc)

    def step(R, inp):
        q_i, k_i, v_i = inp
        cross = jnp.einsum('bhnd,bhde->bhne', q_i.astype(jnp.float32), R) * xi[None, :, :, None]
        R = R * g_chunk[None, :, None, None] + jnp.einsum(
            'bhmd,bhme->bhde', k_i.astype(jnp.float32) * zeta[None, :, :, None],
            v_i.astype(jnp.float32))
        return R, cross

    R0 = jnp.zeros((B, H, dk, dv), jnp.float32)
    _, y_cross = lax.scan(step, R0, (qc.transpose(2, 0, 1, 3, 4),
                                     kc.transpose(2, 0, 1, 3, 4),
                                     vc.transpose(2, 0, 1, 3, 4)))
    y = y_inner.astype(jnp.float32) + y_cross.transpose(1, 2, 0, 3, 4)
    y = y.reshape(B, H, S, dv)
    mu = y.mean(-1, keepdims=True)
    var = jnp.square(y - mu).mean(-1, keepdims=True)
    y = ((y - mu) * lax.rsqrt(var + GN_EPS)).astype(x.dtype)
    y = y.transpose(0, 2, 1, 3).reshape(B, S, H * dv)
    gate = jax.nn.silu(jnp.einsum('bsd,de->bse', x, w_g))
    return jnp.einsum('bse,ed->bsd', gate * y, w_o)


def conv_ffn(x, w_in, conv_w, conv_b, w_out):
    S = x.shape[1]
    u = jnp.einsum('bsd,df->bsf', x, w_in)
    up = jnp.pad(u, ((0, 0), (CONV_WIDTH - 1, 0), (0, 0)))
    c = conv_b + up[:, 0:S] * conv_w[0]
    for j in range(1, CONV_WIDTH):
        c = c + up[:, j:j + S] * conv_w[j]
    g, val = jnp.split(c, 2, axis=-1)
    return jnp.einsum('bsf,fd->bsd', jax.nn.gelu(g) * val, w_out)


def setup_inputs(seed: int = 0) -> dict:
    key = jax.random.key(seed)
    ks = jax.random.split(key, 14)
    D, F = D_MODEL, FFN_DIM
    nrm = lambda k, shape, scale: jax.random.normal(k, shape, jnp.float32) * scale
    return {
        "x": nrm(ks[0], (BATCH, SEQ, D), 1.0),
        "moba_wqkv": nrm(ks[1], (N_MOBA, D, 3 * D), D ** -0.5),
        "moba_wo": nrm(ks[2], (N_MOBA, D, D), D ** -0.5 * DEEPNORM_BETA),
        "ret_wq": nrm(ks[3], (N_RET, D, RET_HEADS * RET_QK_DIM), D ** -0.5),
        "ret_wk": nrm(ks[4], (N_RET, D, RET_HEADS * RET_QK_DIM), D ** -0.5),
        "ret_wv": nrm(ks[5], (N_RET, D, RET_HEADS * RET_V_DIM), D ** -0.5),
        "ret_wg": nrm(ks[6], (N_RET, D, RET_HEADS * RET_V_DIM), D ** -0.5),
        "ret_wo": nrm(ks[7], (N_RET, RET_HEADS * RET_V_DIM, D),
                      (RET_HEADS * RET_V_DIM) ** -0.5 * DEEPNORM_BETA),
        "ffn_w_in": nrm(ks[8], (DEPTH, D, 2 * F), D ** -0.5),
        "ffn_conv_w": nrm(ks[9], (DEPTH, CONV_WIDTH, 2 * F), CONV_WIDTH ** -0.5),
        "ffn_conv_b": nrm(ks[10], (DEPTH, 2 * F), 0.01),
        "ffn_w_out": nrm(ks[11], (DEPTH, F, D), F ** -0.5 * DEEPNORM_BETA),
        "ln_g": 1.0 + nrm(ks[12], (DEPTH, 2, D), 0.02),
        "ln_b": nrm(ks[13], (DEPTH, 2, D), 0.02),
    }


def reference(x, moba_wqkv, moba_wo, ret_wq, ret_wk, ret_wv, ret_wg, ret_wo,
              ffn_w_in, ffn_conv_w, ffn_conv_b, ffn_w_out, ln_g, ln_b):
    B, S, _ = x.shape
    s_pad = -(-S // PAD_MULT) * PAD_MULT
    cos_a, sin_a = rope_tables(s_pad, MOBA_HEAD_DIM)
    cos_r, sin_r = rope_tables(s_pad, RET_QK_DIM)
    h = jnp.pad(x, ((0, 0), (0, s_pad - S), (0, 0)))
    for i in range(DEPTH):
        j = i // N_MIXERS
        if i % N_MIXERS == 0:
            m = moba_mixer(h, moba_wqkv[j], moba_wo[j], cos_a, sin_a)
        else:
            m = retention_mixer(h, ret_wq[j], ret_wk[j], ret_wv[j], ret_wg[j], ret_wo[j],
                                cos_r, sin_r)
        h = layer_norm(DEEPNORM_ALPHA * h + m, ln_g[i, 0], ln_b[i, 0])
        f = conv_ffn(h, ffn_w_in[i], ffn_conv_w[i], ffn_conv_b[i], ffn_w_out[i])
        h = layer_norm(DEEPNORM_ALPHA * h + f, ln_g[i, 1], ln_b[i, 1])
    return h[:, :S]
```

```python
import functools
import math

import jax
import jax.numpy as jnp
from jax import lax
from jax.experimental import pallas as pl
from jax.experimental.pallas import tpu as pltpu

DEPTH = 4
MOBA_HEAD_DIM = 128
MOBA_BLOCK = 256
MOBA_TOPK = 3
ROPE_THETA = 10000.0
RET_QK_DIM = 256
RET_V_DIM = 512
RET_CHUNK = 256
GN_EPS = 1e-5
LN_EPS = 1e-5
CONV_WIDTH = 3
DEEPNORM_ALPHA = (2 * DEPTH) ** 0.25

LANES = 128
SUBLANES = 8
VMEM_LIMIT = 56 * 1024 * 1024

F32 = jnp.float32
BF16 = jnp.bfloat16


def _params(n_axes):
    return pltpu.CompilerParams(
        dimension_semantics=("arbitrary",) * n_axes,
        vmem_limit_bytes=VMEM_LIMIT)


def _rope_tables(n_pos, dim):
    inv = ROPE_THETA ** (-jnp.arange(0, dim, 2, dtype=F32) / dim)
    ang = jnp.arange(n_pos, dtype=F32)[:, None] * inv[None, :]
    return jnp.cos(ang), jnp.sin(ang)


def _rope_half_lane(t, cosf, sinf):
    return t * cosf + pltpu.roll(t, MOBA_HEAD_DIM // 2, axis=1) * sinf


def _moba_proj_kernel(x_ref, w_ref, cos_ref, sin_ref, o_ref, *, n_q_tiles, q_scale):
    n = pl.program_id(0)
    acc = jnp.dot(x_ref[...], w_ref[...], preferred_element_type=F32)
    bn = acc.shape[1]

    @pl.when(n < 2 * n_q_tiles)
    def _():
        scale = jnp.where(n < n_q_tiles, q_scale, 1.0).astype(F32)
        cosf = cos_ref[...]
        sinf = sin_ref[...]
        for g in range(bn // LANES):
            t = acc[:, g * LANES:(g + 1) * LANES]
            o_ref[:, g * LANES:(g + 1) * LANES] = (
                _rope_half_lane(t, cosf, sinf) * scale).astype(o_ref.dtype)

    @pl.when(n >= 2 * n_q_tiles)
    def _():
        o_ref[...] = acc.astype(o_ref.dtype)


def _ret_proj_kernel(x_ref, w_ref, cos_ref, sin_ref, o_ref, *, n_q_tiles, k_scale):
    n = pl.program_id(0)
    acc = jnp.dot(x_ref[...], w_ref[...], preferred_element_type=F32)
    bn = acc.shape[1]

    @pl.when(n < 2 * n_q_tiles)
    def _():
        scale = jnp.where(n < n_q_tiles, 1.0, k_scale).astype(F32)
        c = cos_ref[...]
        s = sin_ref[...]
        for g in range(bn // RET_QK_DIM):
            lo = g * RET_QK_DIM
            mid = lo + RET_QK_DIM // 2
            hi = lo + RET_QK_DIM
            t1 = acc[:, lo:mid]
            t2 = acc[:, mid:hi]
            o_ref[:, lo:mid] = ((t1 * c - t2 * s) * scale).astype(o_ref.dtype)
            o_ref[:, mid:hi] = ((t1 * s + t2 * c) * scale).astype(o_ref.dtype)

    @pl.when(jnp.logical_and(n >= 2 * n_q_tiles, n < 4 * n_q_tiles))
    def _():
        o_ref[...] = acc.astype(o_ref.dtype)

    @pl.when(n >= 4 * n_q_tiles)
    def _():
        o_ref[...] = (acc * jax.nn.sigmoid(acc)).astype(o_ref.dtype)


def _projection(kernel_body, x, w, cos_t, sin_t, seq, bm, bn):
    T, K = x.shape
    N = w.shape[1]
    tiles_per_seq = seq // bm
    return pl.pallas_call(
        kernel_body,
        grid=(N // bn, T // bm),
        in_specs=[
            pl.BlockSpec((bm, K), lambda n, i: (i, 0)),
            pl.BlockSpec((K, bn), lambda n, i: (0, n)),
            pl.BlockSpec((bm, LANES), lambda n, i: (i % tiles_per_seq, 0)),
            pl.BlockSpec((bm, LANES), lambda n, i: (i % tiles_per_seq, 0)),
        ],
        out_specs=pl.BlockSpec((bm, bn), lambda n, i: (i, n)),
        out_shape=jax.ShapeDtypeStruct((T, N), BF16),
        compiler_params=_params(2),
    )(x, w, cos_t, sin_t)


def _mm_ln_kernel(x_ref, w_ref, h_ref, g_ref, b_ref, of_ref, ob_ref, acc_ref, *, nk):
    k = pl.program_id(1)

    @pl.when(k == 0)
    def _():
        acc_ref[...] = jnp.zeros_like(acc_ref)

    acc_ref[...] += jnp.dot(x_ref[...], w_ref[...], preferred_element_type=F32)

    @pl.when(k == nk - 1)
    def _():
        y = DEEPNORM_ALPHA * h_ref[...] + acc_ref[...]
        mu = jnp.mean(y, axis=-1, keepdims=True)
        d = y - mu
        var = jnp.mean(d * d, axis=-1, keepdims=True)
        out = d * lax.rsqrt(var + LN_EPS) * g_ref[...] + b_ref[...]
        of_ref[...] = out
        ob_ref[...] = out.astype(ob_ref.dtype)


def _matmul_residual_ln(x, w, h, g, b, bm, bk):
    T, K = x.shape
    D = w.shape[1]
    nk = K // bk
    return pl.pallas_call(
        functools.partial(_mm_ln_kernel, nk=nk),
        grid=(T // bm, nk),
        in_specs=[
            pl.BlockSpec((bm, bk), lambda i, k: (i, k)),
            pl.BlockSpec((bk, D), lambda i, k: (k, 0)),
            pl.BlockSpec((bm, D), lambda i, k: (i, 0)),
            pl.BlockSpec((1, D), lambda i, k: (0, 0)),
            pl.BlockSpec((1, D), lambda i, k: (0, 0)),
        ],
        out_specs=[
            pl.BlockSpec((bm, D), lambda i, k: (i, 0)),
            pl.BlockSpec((bm, D), lambda i, k: (i, 0)),
        ],
        out_shape=[jax.ShapeDtypeStruct((T, D), F32),
                   jax.ShapeDtypeStruct((T, D), BF16)],
        scratch_shapes=[pltpu.VMEM((bm, D), F32)],
        compiler_params=_params(2),
    )(x, w, h, g.reshape(1, D), b.reshape(1, D))


def _moba_attn_kernel(q_ref, k_ref, v_ref, o_ref, kmean_ref, sel_ref, *, n_blocks):
    i = pl.program_id(2)
    BLK = MOBA_BLOCK
    NEG = -jnp.inf

    @pl.when(i == 0)
    def _():
        def mean_body(j, c):
            start = pl.multiple_of(j * BLK, BLK)
            kb = k_ref[pl.ds(start, BLK), :].astype(F32)
            kmean_ref[pl.ds(j, 1), :] = jnp.mean(kb, axis=0, keepdims=True)
            return c
        lax.fori_loop(0, n_blocks, mean_body, 0)

    q_t = q_ref[...].astype(F32).T.astype(BF16)

    km = kmean_ref[...]
    km_hi = km.astype(BF16)
    km_lo = (km - km_hi.astype(F32)).astype(BF16)
    gate = (jnp.dot(km_hi, q_t, preferred_element_type=F32)
            + jnp.dot(km_lo, q_t, preferred_element_type=F32))
    row = lax.broadcasted_iota(jnp.int32, gate.shape, 0)
    past = row < i
    g = jnp.where(past, gate, NEG)
    chosen = jnp.zeros(gate.shape, jnp.bool_)
    for _ in range(min(MOBA_TOPK, n_blocks)):
        best = jnp.max(g, axis=0, keepdims=True)
        first = jnp.min(jnp.where(g == best, row, n_blocks), axis=0, keepdims=True)
        pick = row == first
        chosen = jnp.logical_or(chosen, pick)
        g = jnp.where(pick, NEG, g)
    sel_ref[...] = jnp.where(jnp.logical_and(chosen, past), 1.0, 0.0).astype(F32)

    def attend(start, mask, carry):
        m, l, acc = carry
        kj = k_ref[pl.ds(start, BLK), :]
        vj = v_ref[pl.ds(start, BLK), :]
        s = jnp.dot(kj, q_t, preferred_element_type=F32)
        s = jnp.where(mask, s, NEG)
        m_new = jnp.maximum(m, jnp.max(s, axis=0, keepdims=True))
        a = jnp.exp(m - m_new)
        p = jnp.exp(s - m_new)
        l = a * l + jnp.sum(p, axis=0, keepdims=True)
        pv = lax.dot_general(vj, p.astype(BF16), (((0,), (0,)), ((), ())),
                             preferred_element_type=F32)
        return m_new, l, a * acc + pv

    kpos = lax.broadcasted_iota(jnp.int32, (BLK, BLK), 0)
    qpos = lax.broadcasted_iota(jnp.int32, (BLK, BLK), 1)
    init = (jnp.full((1, BLK), NEG, F32), jnp.zeros((1, BLK), F32),
            jnp.zeros((MOBA_HEAD_DIM, BLK), F32))
    carry = attend(pl.multiple_of(i * BLK, BLK), kpos <= qpos, init)

    def past_body(j, carry):
        mask = sel_ref[pl.ds(j, 1), :] > 0.5
        return attend(pl.multiple_of(j * BLK, BLK), mask, carry)

    m, l, acc = lax.fori_loop(0, i, past_body, carry)
    o_ref[...] = (acc / l).T.astype(o_ref.dtype)


def _moba_attention(qkv, n_heads):
    B, S, _ = qkv.shape
    Dh, BLK = MOBA_HEAD_DIM, MOBA_BLOCK
    nb = S // BLK
    return pl.pallas_call(
        functools.partial(_moba_attn_kernel, n_blocks=nb),
        grid=(B, n_heads, nb),
        in_specs=[
            pl.BlockSpec((None, BLK, Dh), lambda b, h, i: (b, i, h)),
            pl.BlockSpec((None, S, Dh), lambda b, h, i: (b, 0, n_heads + h)),
            pl.BlockSpec((None, S, Dh), lambda b, h, i: (b, 0, 2 * n_heads + h)),
        ],
        out_specs=pl.BlockSpec((None, BLK, Dh), lambda b, h, i: (b, i, h)),
        out_shape=jax.ShapeDtypeStruct((B, S, n_heads * Dh), BF16),
        scratch_shapes=[pltpu.VMEM((nb, Dh), F32), pltpu.VMEM((nb, BLK), F32)],
        compiler_params=_params(3),
    )(qkv, qkv, qkv)


def _retention_kernel(q_ref, k_ref, v_ref, g_ref, decay_ref, xi_ref, zeta_ref,
                      gch_ref, o_ref, state_ref):
    c = pl.program_id(2)

    @pl.when(c == 0)
    def _():
        state_ref[...] = jnp.zeros_like(state_ref)

    q = q_ref[...]
    k = k_ref[...]
    v = v_ref[...]
    s = lax.dot_general(q, k, (((1,), (1,)), ((), ())), preferred_element_type=F32)
    s = s * decay_ref[...]
    y = jnp.dot(s.astype(BF16), v, preferred_element_type=F32)
    state = state_ref[...]
    y = y + jnp.dot(q, state.astype(BF16), preferred_element_type=F32) * xi_ref[...]
    kz = (k.astype(F32) * zeta_ref[...]).astype(BF16)
    state_ref[...] = state * gch_ref[...] + lax.dot_general(
        kz, v, (((0,), (0,)), ((), ())), preferred_element_type=F32)
    mu = jnp.mean(y, axis=-1, keepdims=True)
    d = y - mu
    var = jnp.mean(d * d, axis=-1, keepdims=True)
    yn = d * lax.rsqrt(var + GN_EPS)
    o_ref[...] = (g_ref[...].astype(F32) * yn).astype(o_ref.dtype)


def _retention(proj, n_heads):
    B, S, _ = proj.shape
    H, dk, dv, L = n_heads, RET_QK_DIM, RET_V_DIM, RET_CHUNK
    nc = S // L
    log_gamma = jnp.log1p(-jnp.exp2(-5.0 - jnp.arange(H, dtype=F32)))
    pos = jnp.arange(L, dtype=F32)
    diff = pos[:, None] - pos[None, :]
    decay = jnp.where(diff >= 0,
                      jnp.exp(jnp.maximum(diff, 0.0)[None] * log_gamma[:, None, None]),
                      0.0)
    xi = jnp.exp((pos[None, :] + 1.0) * log_gamma[:, None])[..., None]
    zeta = jnp.exp((L - 1.0 - pos[None, :]) * log_gamma[:, None])[..., None]
    g_chunk = jnp.exp(L * log_gamma)[:, None, None]
    k_off = H * dk // dk
    v_off = 2 * H * dk // dv
    g_off = v_off + H
    return pl.pallas_call(
        _retention_kernel,
        grid=(B, H, nc),
        in_specs=[
            pl.BlockSpec((None, L, dk), lambda b, h, c: (b, c, h)),
            pl.BlockSpec((None, L, dk), lambda b, h, c: (b, c, k_off + h)),
            pl.BlockSpec((None, L, dv), lambda b, h, c: (b, c, v_off + h)),
            pl.BlockSpec((None, L, dv), lambda b, h, c: (b, c, g_off + h)),
            pl.BlockSpec((None, L, L), lambda b, h, c: (h, 0, 0)),
            pl.BlockSpec((None, L, 1), lambda b, h, c: (h, 0, 0)),
            pl.BlockSpec((None, L, 1), lambda b, h, c: (h, 0, 0)),
            pl.BlockSpec((None, 1, 1), lambda b, h, c: (h, 0, 0)),
        ],
        out_specs=pl.BlockSpec((None, L, dv), lambda b, h, c: (b, c, h)),
        out_shape=jax.ShapeDtypeStruct((B, S, H * dv), BF16),
        scratch_shapes=[pltpu.VMEM((dk, dv), F32)],
        compiler_params=_params(3),
    )(proj, proj, proj, proj, decay, xi, zeta, g_chunk)


def _ffn_up_kernel(x_ref, wg_ref, wv_ref, cwg_ref, cwv_ref, cbg_ref, cbv_ref,
                   o_ref, ext_ref, carry_ref, *, tiles_per_seq):
    i = pl.program_id(1)
    x = x_ref[...]
    bm = x.shape[0]
    halo = SUBLANES

    @pl.when(i % tiles_per_seq == 0)
    def _():
        carry_ref[...] = jnp.zeros_like(carry_ref)

    def conv(w_ref, cw_ref, cb_ref, slot):
        u = jnp.dot(x, w_ref[...], preferred_element_type=F32)
        ext_ref[0:halo, :] = carry_ref[slot]
        ext_ref[halo:halo + bm, :] = u
        carry_ref[slot] = u[bm - halo:bm, :]
        u1 = ext_ref[halo - 1:halo - 1 + bm, :]
        u2 = ext_ref[halo - 2:halo - 2 + bm, :]
        return (cb_ref[...] + u2 * cw_ref[0:1, :] + u1 * cw_ref[1:2, :]
                + u * cw_ref[2:3, :])

    gate = conv(wg_ref, cwg_ref, cbg_ref, 0)
    val = conv(wv_ref, cwv_ref, cbv_ref, 1)
    o_ref[...] = (jax.nn.gelu(gate) * val).astype(o_ref.dtype)


def _ffn_up(x, w_in, conv_w, conv_b, seq, bm, bn):
    T, D = x.shape
    F = w_in.shape[1] // 2
    nf = F // bn
    tiles_per_seq = seq // bm
    conv_b = conv_b.reshape(1, 2 * F)
    return pl.pallas_call(
        functools.partial(_ffn_up_kernel, tiles_per_seq=tiles_per_seq),
        grid=(nf, T // bm),
        in_specs=[
            pl.BlockSpec((bm, D), lambda j, i: (i, 0)),
            pl.BlockSpec((D, bn), lambda j, i: (0, j)),
            pl.BlockSpec((D, bn), lambda j, i: (0, nf + j)),
            pl.BlockSpec((CONV_WIDTH, bn), lambda j, i: (0, j)),
            pl.BlockSpec((CONV_WIDTH, bn), lambda j, i: (0, nf + j)),
            pl.BlockSpec((1, bn), lambda j, i: (0, j)),
            pl.BlockSpec((1, bn), lambda j, i: (0, nf + j)),
        ],
        out_specs=pl.BlockSpec((bm, bn), lambda j, i: (i, j)),
        out_shape=jax.ShapeDtypeStruct((T, F), BF16),
        scratch_shapes=[pltpu.VMEM((bm + SUBLANES, bn), F32),
                        pltpu.VMEM((2, SUBLANES, bn), F32)],
        compiler_params=_params(2),
    )(x, w_in, w_in, conv_w, conv_w, conv_b, conv_b)


def _largest_divisor(n, cap, multiple):
    best = None
    for d in range(multiple, min(n, cap) + 1, multiple):
        if n % d == 0:
            best = d
    assert best is not None, (n, cap, multiple)
    return best


def kernel(x, moba_wqkv, moba_wo, ret_wq, ret_wk, ret_wv, ret_wg, ret_wo,
           ffn_w_in, ffn_conv_w, ffn_conv_b, ffn_w_out, ln_g, ln_b):
    B, S, D = x.shape
    T = B * S
    assert S % MOBA_BLOCK == 0 and S % RET_CHUNK == 0
    moba_heads = D // MOBA_HEAD_DIM
    ret_heads = ret_wq.shape[2] // RET_QK_DIM
    assert ret_wv.shape[2] == ret_heads * RET_V_DIM
    F = ffn_w_out.shape[1]

    bm = _largest_divisor(S, 1024, MOBA_BLOCK)
    bn = 1024
    bm_ln = _largest_divisor(S, 512, MOBA_BLOCK)
    bm_ffn = _largest_divisor(S, 512, MOBA_BLOCK)
    bn_ffn = _largest_divisor(F, 512, LANES)
    bk_ffn = _largest_divisor(F, 1536, LANES)

    cos_a, sin_a = _rope_tables(S, MOBA_HEAD_DIM)
    cos_a = jnp.concatenate([cos_a, cos_a], axis=-1)
    sin_a = jnp.concatenate([-sin_a, sin_a], axis=-1)
    cos_r, sin_r = _rope_tables(S, RET_QK_DIM)

    h = x.reshape(T, D)
    hb = h.astype(BF16)
    for i in range(DEPTH):
        j = i // 2
        if i % 2 == 0:
            qkv = _projection(
                functools.partial(_moba_proj_kernel, n_q_tiles=D // bn,
                                  q_scale=MOBA_HEAD_DIM ** -0.5),
                hb, moba_wqkv[j].astype(BF16), cos_a, sin_a, S, bm, bn)
            mixed = _moba_attention(qkv.reshape(B, S, 3 * D), moba_heads).reshape(T, D)
            w_o = moba_wo[j].astype(BF16)
        else:
            w_cat = jnp.concatenate([ret_wq[j], ret_wk[j], ret_wv[j], ret_wg[j]],
                                    axis=1).astype(BF16)
            proj = _projection(
                functools.partial(_ret_proj_kernel, n_q_tiles=ret_heads * RET_QK_DIM // bn,
                                  k_scale=RET_QK_DIM ** -0.5),
                hb, w_cat, cos_r, sin_r, S, bm, bn)
            mixed = _retention(proj.reshape(B, S, -1), ret_heads).reshape(T, -1)
            w_o = ret_wo[j].astype(BF16)
        h, hb = _matmul_residual_ln(mixed, w_o, h, ln_g[i, 0], ln_b[i, 0],
                                    bm_ln, min(w_o.shape[0], 2048))
        act = _ffn_up(hb, ffn_w_in[i].astype(BF16), ffn_conv_w[i], ffn_conv_b[i],
                      S, bm_ffn, bn_ffn)
        h, hb = _matmul_residual_ln(act, ffn_w_out[i].astype(BF16), h,
                                    ln_g[i, 1], ln_b[i, 1], bm_ln, bk_ffn)
    return h.reshape(B, S, D)
```

```python
import functools
import math

import jax
import jax.numpy as jnp
from jax import lax
from jax.experimental import pallas as pl
from jax.experimental.pallas import tpu as pltpu

DEPTH = 4
MOBA_HEAD_DIM = 128
MOBA_BLOCK = 256
MOBA_TOPK = 3
ROPE_THETA = 10000.0
RET_QK_DIM = 256
RET_V_DIM = 512
RET_CHUNK = 256
GN_EPS = 1e-5
LN_EPS = 1e-5
CONV_WIDTH = 3
MOBA_HEAD_GROUP = 4
DEEPNORM_ALPHA = (2 * DEPTH) ** 0.25

LANES = 128
SUBLANES = 8
VMEM_LIMIT = 56 * 1024 * 1024

F32 = jnp.float32
BF16 = jnp.bfloat16


def _params(n_axes):
    return pltpu.CompilerParams(
        dimension_semantics=("arbitrary",) * n_axes,
        vmem_limit_bytes=VMEM_LIMIT)


def _rope_tables(n_pos, dim):
    inv = ROPE_THETA ** (-jnp.arange(0, dim, 2, dtype=jnp.float32) / dim)
    ang = jnp.arange(n_pos, dtype=jnp.float32)[:, None] * inv[None, :]
    return jnp.cos(ang), jnp.sin(ang)


def _rope_half_lane(t, cosf, sinf):
    return t * cosf + pltpu.roll(t, MOBA_HEAD_DIM // 2, axis=1) * sinf


def _moba_proj_kernel(x_ref, w_ref, cos_ref, sin_ref, o_ref, *, n_q_tiles, q_scale):
    n = pl.program_id(0)
    acc = jnp.dot(x_ref[...], w_ref[...], preferred_element_type=F32)
    bn = acc.shape[1]

    @pl.when(n < 2 * n_q_tiles)
    def _():
        scale = jnp.where(n < n_q_tiles, q_scale, 1.0).astype(F32)
        cosf = cos_ref[...]
        sinf = sin_ref[...]
        for g in range(bn // LANES):
            t = acc[:, g * LANES:(g + 1) * LANES]
            o_ref[:, g * LANES:(g + 1) * LANES] = (
                _rope_half_lane(t, cosf, sinf) * scale).astype(o_ref.dtype)

    @pl.when(n >= 2 * n_q_tiles)
    def _():
        o_ref[...] = acc.astype(o_ref.dtype)


def _ret_proj_kernel(x_ref, w_ref, cos_ref, sin_ref, o_ref, *, n_q_tiles, k_scale):
    n = pl.program_id(0)
    acc = jnp.dot(x_ref[...], w_ref[...], preferred_element_type=F32)
    bn = acc.shape[1]

    @pl.when(n < 2 * n_q_tiles)
    def _():
        scale = jnp.where(n < n_q_tiles, 1.0, k_scale).astype(F32)
        c = cos_ref[...]
        s = sin_ref[...]
        for g in range(bn // RET_QK_DIM):
            lo = g * RET_QK_DIM
            mid = lo + RET_QK_DIM // 2
            hi = lo + RET_QK_DIM
            t1 = acc[:, lo:mid]
            t2 = acc[:, mid:hi]
            o_ref[:, lo:mid] = ((t1 * c - t2 * s) * scale).astype(o_ref.dtype)
            o_ref[:, mid:hi] = ((t1 * s + t2 * c) * scale).astype(o_ref.dtype)

    @pl.when(jnp.logical_and(n >= 2 * n_q_tiles, n < 4 * n_q_tiles))
    def _():
        o_ref[...] = acc.astype(o_ref.dtype)

    @pl.when(n >= 4 * n_q_tiles)
    def _():
        o_ref[...] = (acc * jax.nn.sigmoid(acc)).astype(o_ref.dtype)


def _projection(kernel_body, name, x, w, cos_t, sin_t, seq, bm, bn):
    T, K = x.shape
    N = w.shape[1]
    tiles_per_seq = seq // bm
    return pl.pallas_call(
        kernel_body,
        grid=(N // bn, T // bm),
        in_specs=[
            pl.BlockSpec((bm, K), lambda n, i: (i, 0)),
            pl.BlockSpec((K, bn), lambda n, i: (0, n)),
            pl.BlockSpec((bm, LANES), lambda n, i: (i % tiles_per_seq, 0)),
            pl.BlockSpec((bm, LANES), lambda n, i: (i % tiles_per_seq, 0)),
        ],
        out_specs=pl.BlockSpec((bm, bn), lambda n, i: (i, n)),
        out_shape=jax.ShapeDtypeStruct((T, N), BF16),
        compiler_params=_params(2),
        name=name,
    )(x, w, cos_t, sin_t)


def _mm_ln_kernel(x_ref, w_ref, h_ref, g_ref, b_ref, of_ref, ob_ref, acc_ref, *, nk):
    k = pl.program_id(1)

    @pl.when(k == 0)
    def _():
        acc_ref[...] = jnp.zeros_like(acc_ref)

    acc_ref[...] += jnp.dot(x_ref[...], w_ref[...], preferred_element_type=F32)

    @pl.when(k == nk - 1)
    def _():
        y = DEEPNORM_ALPHA * h_ref[...] + acc_ref[...]
        mu = jnp.mean(y, axis=-1, keepdims=True)
        d = y - mu
        var = jnp.mean(d * d, axis=-1, keepdims=True)
        out = d * lax.rsqrt(var + LN_EPS) * g_ref[...] + b_ref[...]
        of_ref[...] = out
        ob_ref[...] = out.astype(ob_ref.dtype)


def _matmul_residual_ln(x, w, h, g, b, bm, bk):
    T, K = x.shape
    D = w.shape[1]
    nk = K // bk
    return pl.pallas_call(
        functools.partial(_mm_ln_kernel, nk=nk),
        grid=(T // bm, nk),
        in_specs=[
            pl.BlockSpec((bm, bk), lambda i, k: (i, k)),
            pl.BlockSpec((bk, D), lambda i, k: (k, 0)),
            pl.BlockSpec((bm, D), lambda i, k: (i, 0)),
            pl.BlockSpec((1, D), lambda i, k: (0, 0)),
            pl.BlockSpec((1, D), lambda i, k: (0, 0)),
        ],
        out_specs=[
            pl.BlockSpec((bm, D), lambda i, k: (i, 0)),
            pl.BlockSpec((bm, D), lambda i, k: (i, 0)),
        ],
        out_shape=[jax.ShapeDtypeStruct((T, D), F32),
                   jax.ShapeDtypeStruct((T, D), BF16)],
        scratch_shapes=[pltpu.VMEM((bm, D), F32)],
        compiler_params=_params(2),
        name="matmul_residual_ln",
    )(x, w, h, g.reshape(1, D), b.reshape(1, D))


def _moba_attn_kernel(q_ref, k_ref, v_ref, o_ref, kmean_ref, bias_ref, s_ref, p_ref,
                      a_ref, m_ref, l_ref, acc_ref, *, n_blocks, n_group):
    i = pl.program_id(2)
    BLK, Dh = MOBA_BLOCK, MOBA_HEAD_DIM
    NEG = -jnp.inf
    heads = range(n_group)

    def head_cols(g):
        return slice(g * Dh, (g + 1) * Dh)

    @pl.when(i == 0)
    def _():
        def mean_body(j, c):
            start = pl.multiple_of(j * BLK, BLK)
            kb = k_ref[pl.ds(start, BLK), :].astype(F32)
            mean = jnp.mean(kb, axis=0, keepdims=True)
            for g in heads:
                kmean_ref[g, pl.ds(j, 1), :] = mean[:, head_cols(g)]
            return c
        lax.fori_loop(0, n_blocks, mean_body, 0)

    q_ts = []
    for g in heads:
        q_t = q_ref[:, head_cols(g)].astype(F32).T.astype(BF16)
        q_ts.append(q_t)
        km = kmean_ref[g]
        km_hi = km.astype(BF16)
        km_lo = (km - km_hi.astype(F32)).astype(BF16)
        gate = (jnp.dot(km_hi, q_t, preferred_element_type=F32)
                + jnp.dot(km_lo, q_t, preferred_element_type=F32))
        row = lax.broadcasted_iota(jnp.int32, gate.shape, 0)
        past = row < i
        gm = jnp.where(past, gate, NEG)
        chosen = jnp.zeros(gate.shape, jnp.bool_)
        for _ in range(min(MOBA_TOPK, n_blocks)):
            best = jnp.max(gm, axis=0, keepdims=True)
            first = jnp.min(jnp.where(gm == best, row, n_blocks), axis=0, keepdims=True)
            pick = row == first
            chosen = jnp.logical_or(chosen, pick)
            gm = jnp.where(pick, NEG, gm)
        bias_ref[g] = jnp.where(jnp.logical_and(chosen, past), 0.0, NEG).astype(F32)

    def scores(g, start):
        return jnp.dot(k_ref[pl.ds(start, BLK), head_cols(g)], q_ts[g],
                       preferred_element_type=F32)

    def weighted_values(g, start, p):
        return lax.dot_general(v_ref[pl.ds(start, BLK), head_cols(g)], p,
                               (((0,), (0,)), ((), ())),
                               preferred_element_type=F32)

    kpos = lax.broadcasted_iota(jnp.int32, (BLK, BLK), 0)
    qpos = lax.broadcasted_iota(jnp.int32, (BLK, BLK), 1)
    causal = kpos <= qpos
    own = pl.multiple_of(i * BLK, BLK)
    for g in heads:
        s_ref[0, g] = jnp.where(causal, scores(g, own), NEG)
        p_ref[0, g] = jnp.zeros((BLK, BLK), BF16)
        a_ref[g] = jnp.ones((1, BLK), F32)
        m_ref[g] = jnp.full((1, BLK), NEG, F32)
        l_ref[g] = jnp.zeros((1, BLK), F32)
        acc_ref[g] = jnp.zeros((Dh, BLK), F32)

    def trip(t, c):
        cur = t & 1
        nxt = 1 - cur
        j_next = jnp.minimum(t, jnp.maximum(i - 1, 0))
        start_next = pl.multiple_of(j_next * BLK, BLK)
        j_prev = jnp.where(t == 1, i, jnp.maximum(t - 2, 0))
        start_prev = pl.multiple_of(j_prev * BLK, BLK)
        s_next = [scores(g, start_next) + bias_ref[g, pl.ds(j_next, 1), :] for g in heads]
        pv = [weighted_values(g, start_prev, p_ref[cur, g]) for g in heads]
        for g in heads:
            s = s_ref[cur, g]
            m = m_ref[g]
            m_new = jnp.maximum(m, jnp.max(s, axis=0, keepdims=True))
            a = jnp.exp(m - m_new)
            p = jnp.exp(s - m_new)
            l_ref[g] = a * l_ref[g] + jnp.sum(p, axis=0, keepdims=True)
            m_ref[g] = m_new
            acc_ref[g] = a_ref[g] * acc_ref[g] + pv[g]
            a_ref[g] = a
            p_ref[nxt, g] = p.astype(BF16)
            s_ref[nxt, g] = s_next[g]
        return c

    lax.fori_loop(0, i + 1, trip, 0)

    last = (i + 1) & 1
    start_last = pl.multiple_of(jnp.maximum(i - 1, 0) * BLK, BLK)
    for g in heads:
        acc = a_ref[g] * acc_ref[g] + weighted_values(g, start_last, p_ref[last, g])
        o_ref[:, head_cols(g)] = (acc / l_ref[g]).T.astype(o_ref.dtype)


def _moba_attention(qkv, n_heads, n_group):
    B, S, _ = qkv.shape
    Dh, BLK = MOBA_HEAD_DIM, MOBA_BLOCK
    nb = S // BLK
    n_hg = n_heads // n_group
    W = n_group * Dh
    return pl.pallas_call(
        functools.partial(_moba_attn_kernel, n_blocks=nb, n_group=n_group),
        grid=(B, n_hg, nb),
        in_specs=[
            pl.BlockSpec((None, BLK, W), lambda b, h, i: (b, i, h)),
            pl.BlockSpec((None, S, W), lambda b, h, i: (b, 0, n_hg + h)),
            pl.BlockSpec((None, S, W), lambda b, h, i: (b, 0, 2 * n_hg + h)),
        ],
        out_specs=pl.BlockSpec((None, BLK, W), lambda b, h, i: (b, i, h)),
        out_shape=jax.ShapeDtypeStruct((B, S, n_heads * Dh), BF16),
        scratch_shapes=[pltpu.VMEM((n_group, nb, Dh), F32),
                        pltpu.VMEM((n_group, nb, BLK), F32),
                        pltpu.VMEM((2, n_group, BLK, BLK), F32),
                        pltpu.VMEM((2, n_group, BLK, BLK), BF16),
                        pltpu.VMEM((n_group, 1, BLK), F32),
                        pltpu.VMEM((n_group, 1, BLK), F32),
                        pltpu.VMEM((n_group, 1, BLK), F32),
                        pltpu.VMEM((n_group, Dh, BLK), F32)],
        compiler_params=_params(3),
        name="moba_attention",
    )(qkv, qkv, qkv)


def _retention_kernel(q_ref, k_ref, v_ref, g_ref, decay_ref, xi_ref, zeta_ref,
                      gch_ref, o_ref, state_ref):
    c = pl.program_id(2)

    @pl.when(c == 0)
    def _():
        state_ref[...] = jnp.zeros_like(state_ref)

    q = q_ref[...]
    k = k_ref[...]
    v = v_ref[...]
    s = lax.dot_general(q, k, (((1,), (1,)), ((), ())), preferred_element_type=F32)
    s = s * decay_ref[...]
    y = jnp.dot(s.astype(BF16), v, preferred_element_type=F32)
    state = state_ref[...]
    y = y + jnp.dot(q, state.astype(BF16), preferred_element_type=F32) * xi_ref[...]
    kz = (k.astype(F32) * zeta_ref[...]).astype(BF16)
    state_ref[...] = state * gch_ref[...] + lax.dot_general(
        kz, v, (((0,), (0,)), ((), ())), preferred_element_type=F32)
    mu = jnp.mean(y, axis=-1, keepdims=True)
    d = y - mu
    var = jnp.mean(d * d, axis=-1, keepdims=True)
    yn = d * lax.rsqrt(var + GN_EPS)
    o_ref[...] = (g_ref[...].astype(F32) * yn).astype(o_ref.dtype)


def _retention(proj, n_heads):
    B, S, _ = proj.shape
    H, dk, dv, L = n_heads, RET_QK_DIM, RET_V_DIM, RET_CHUNK
    nc = S // L
    log_gamma = jnp.log1p(-jnp.exp2(-5.0 - jnp.arange(H, dtype=jnp.float32)))
    pos = jnp.arange(L, dtype=jnp.float32)
    diff = pos[:, None] - pos[None, :]
    decay = jnp.where(diff >= 0,
                      jnp.exp(jnp.maximum(diff, 0.0)[None] * log_gamma[:, None, None]),
                      0.0)
    xi = jnp.exp((pos[None, :] + 1.0) * log_gamma[:, None])[..., None]
    zeta = jnp.exp((L - 1.0 - pos[None, :]) * log_gamma[:, None])[..., None]
    g_chunk = jnp.exp(L * log_gamma)[:, None, None]
    k_off = H * dk // dk
    v_off = 2 * H * dk // dv
    g_off = v_off + H
    return pl.pallas_call(
        _retention_kernel,
        grid=(B, H, nc),
        in_specs=[
            pl.BlockSpec((None, L, dk), lambda b, h, c: (b, c, h)),
            pl.BlockSpec((None, L, dk), lambda b, h, c: (b, c, k_off + h)),
            pl.BlockSpec((None, L, dv), lambda b, h, c: (b, c, v_off + h)),
            pl.BlockSpec((None, L, dv), lambda b, h, c: (b, c, g_off + h)),
            pl.BlockSpec((None, L, L), lambda b, h, c: (h, 0, 0)),
            pl.BlockSpec((None, L, 1), lambda b, h, c: (h, 0, 0)),
            pl.BlockSpec((None, L, 1), lambda b, h, c: (h, 0, 0)),
            pl.BlockSpec((None, 1, 1), lambda b, h, c: (h, 0, 0)),
        ],
        out_specs=pl.BlockSpec((None, L, dv), lambda b, h, c: (b, c, h)),
        out_shape=jax.ShapeDtypeStruct((B, S, H * dv), BF16),
        scratch_shapes=[pltpu.VMEM((dk, dv), F32)],
        compiler_params=_params(3),
        name="retention",
    )(proj, proj, proj, proj, decay, xi, zeta, g_chunk)


def _ffn_up_kernel(x_ref, wg_ref, wv_ref, cwg_ref, cwv_ref, cbg_ref, cbv_ref,
                   o_ref, ext_ref, carry_ref, *, tiles_per_seq):
    i = pl.program_id(1)
    x = x_ref[...]
    bm = x.shape[0]
    halo = SUBLANES

    @pl.when(i % tiles_per_seq == 0)
    def _():
        carry_ref[...] = jnp.zeros_like(carry_ref)

    def conv(w_ref, cw_ref, cb_ref, slot):
        u = jnp.dot(x, w_ref[...], preferred_element_type=F32)
        ext_ref[0:halo, :] = carry_ref[slot]
        ext_ref[halo:halo + bm, :] = u
        carry_ref[slot] = u[bm - halo:bm, :]
        u1 = ext_ref[halo - 1:halo - 1 + bm, :]
        u2 = ext_ref[halo - 2:halo - 2 + bm, :]
        return (cb_ref[...] + u2 * cw_ref[0:1, :] + u1 * cw_ref[1:2, :]
                + u * cw_ref[2:3, :])

    gate = conv(wg_ref, cwg_ref, cbg_ref, 0)
    val = conv(wv_ref, cwv_ref, cbv_ref, 1)
    o_ref[...] = (jax.nn.gelu(gate) * val).astype(o_ref.dtype)


def _ffn_up(x, w_in, conv_w, conv_b, seq, bm, bn):
    T, D = x.shape
    F = w_in.shape[1] // 2
    nf = F // bn
    tiles_per_seq = seq // bm
    conv_b = conv_b.reshape(1, 2 * F)
    return pl.pallas_call(
        functools.partial(_ffn_up_kernel, tiles_per_seq=tiles_per_seq),
        grid=(nf, T // bm),
        in_specs=[
            pl.BlockSpec((bm, D), lambda j, i: (i, 0)),
            pl.BlockSpec((D, bn), lambda j, i: (0, j)),
            pl.BlockSpec((D, bn), lambda j, i: (0, nf + j)),
            pl.BlockSpec((CONV_WIDTH, bn), lambda j, i: (0, j)),
            pl.BlockSpec((CONV_WIDTH, bn), lambda j, i: (0, nf + j)),
            pl.BlockSpec((1, bn), lambda j, i: (0, j)),
            pl.BlockSpec((1, bn), lambda j, i: (0, nf + j)),
        ],
        out_specs=pl.BlockSpec((bm, bn), lambda j, i: (i, j)),
        out_shape=jax.ShapeDtypeStruct((T, F), BF16),
        scratch_shapes=[pltpu.VMEM((bm + SUBLANES, bn), F32),
                        pltpu.VMEM((2, SUBLANES, bn), F32)],
        compiler_params=_params(2),
        name="ffn_up_conv_gate",
    )(x, w_in, w_in, conv_w, conv_w, conv_b, conv_b)


def _largest_divisor(n, cap, multiple):
    best = None
    for d in range(multiple, min(n, cap) + 1, multiple):
        if n % d == 0:
            best = d
    assert best is not None, (n, cap, multiple)
    return best


def kernel(x, moba_wqkv, moba_wo, ret_wq, ret_wk, ret_wv, ret_wg, ret_wo,
           ffn_w_in, ffn_conv_w, ffn_conv_b, ffn_w_out, ln_g, ln_b):
    B, S, D = x.shape
    T = B * S
    assert S % MOBA_BLOCK == 0 and S % RET_CHUNK == 0
    moba_heads = D // MOBA_HEAD_DIM
    ret_heads = ret_wq.shape[2] // RET_QK_DIM
    assert ret_wv.shape[2] == ret_heads * RET_V_DIM
    F = ffn_w_out.shape[1]

    bm = _largest_divisor(S, 1024, MOBA_BLOCK)
    bn = 1024
    bm_ln = _largest_divisor(S, 512, MOBA_BLOCK)
    bm_ffn = _largest_divisor(S, 512, MOBA_BLOCK)
    bn_ffn = _largest_divisor(F, 512, LANES)
    bk_ffn = _largest_divisor(F, 1536, LANES)

    cos_a, sin_a = _rope_tables(S, MOBA_HEAD_DIM)
    cos_a = jnp.concatenate([cos_a, cos_a], axis=-1)
    sin_a = jnp.concatenate([-sin_a, sin_a], axis=-1)
    cos_r, sin_r = _rope_tables(S, RET_QK_DIM)

    h = x.reshape(T, D)
    hb = h.astype(BF16)
    for i in range(DEPTH):
        j = i // 2
        if i % 2 == 0:
            qkv = _projection(
                functools.partial(_moba_proj_kernel, n_q_tiles=D // bn,
                                  q_scale=MOBA_HEAD_DIM ** -0.5),
                "moba_qkv_proj", hb, moba_wqkv[j].astype(BF16), cos_a, sin_a, S, bm, bn)
            mixed = _moba_attention(qkv.reshape(B, S, 3 * D), moba_heads,
                                    MOBA_HEAD_GROUP).reshape(T, D)
            w_o = moba_wo[j].astype(BF16)
        else:
            w_cat = jnp.concatenate([ret_wq[j], ret_wk[j], ret_wv[j], ret_wg[j]],
                                    axis=1).astype(BF16)
            proj = _projection(
                functools.partial(_ret_proj_kernel, n_q_tiles=ret_heads * RET_QK_DIM // bn,
                                  k_scale=RET_QK_DIM ** -0.5),
                "retention_proj", hb, w_cat, cos_r, sin_r, S, bm, bn)
            mixed = _retention(proj.reshape(B, S, -1), ret_heads).reshape(T, -1)
            w_o = ret_wo[j].astype(BF16)
        h, hb = _matmul_residual_ln(mixed, w_o, h, ln_g[i, 0], ln_b[i, 0],
                                    bm_ln, min(w_o.shape[0], 2048))
        act = _ffn_up(hb, ffn_w_in[i].astype(BF16), ffn_conv_w[i], ffn_conv_b[i],
                      S, bm_ffn, bn_ffn)
        h, hb = _matmul_residual_ln(act, ffn_w_out[i].astype(BF16), h,
                                    ln_g[i, 1], ln_b[i, 1], bm_ln, bk_ffn)
    return h.reshape(B, S, D)
```
